```python
import jax, jax.numpy as jnp
from jax import lax
import numpy as np

D_MODEL = 1024
BATCH = 8
SEQ = 4096
DEPTH = 2

GRID_W = 64
CTX_LEN = 256
Q_BLOCK = 128
ROPE_THETA = 10000.0
EPS = 1e-6

GQA_HEADS = 16
GQA_KV_HEADS = 4
GQA_HEAD_DIM = D_MODEL // GQA_HEADS
GQA_GROUP = GQA_HEADS // GQA_KV_HEADS

MLA_HEADS = 16
MLA_NOPE_DIM = 64
MLA_ROPE_DIM = 32
MLA_V_DIM = 64
MLA_Q_RANK = 384
MLA_KV_RANK = 256
MLA_QK_DIM = MLA_NOPE_DIM + MLA_ROPE_DIM

D_FF = -(-8 * D_MODEL // (3 * 256)) * 256

N_MIXERS = 2
N_A = (DEPTH + 1) // 2
N_B = DEPTH // 2

kernel_name = "hybrid_gqa_mla_prefix_dit"


def rmsnorm(x, g):
    x32 = x.astype(jnp.float32)
    y = x32 * lax.rsqrt(jnp.mean(x32 * x32, axis=-1, keepdims=True) + EPS)
    return (y * g.astype(jnp.float32)).astype(x.dtype)


def modulate(h, shift, scale):
    return h * (1 + scale) + shift


def axial_rope(rows, rot_dim):
    n = rot_dim // 4
    inv = ROPE_THETA ** (-jnp.arange(n, dtype=jnp.float32) / n)
    row = jnp.repeat(jnp.arange(rows, dtype=jnp.float32), GRID_W)
    col = jnp.tile(jnp.arange(GRID_W, dtype=jnp.float32), rows)
    ang = jnp.concatenate([row[:, None] * inv, col[:, None] * inv], axis=-1)
    return jnp.cos(ang), jnp.sin(ang)


def apply_rope(x, cos, sin):
    half = x.shape[-1] // 2
    x32 = x.astype(jnp.float32)
    x1, x2 = x32[..., :half], x32[..., half:]
    c = cos[None, :, None, :]
    s = sin[None, :, None, :]
    return jnp.concatenate([x1 * c - x2 * s, x2 * c + x1 * s], axis=-1).astype(x.dtype)


def attend_blocks(q, k, v, scale):
    B, S, Hkv, G, dq = q.shape
    nb = S // Q_BLOCK
    qb = q.reshape(B, nb, Q_BLOCK, Hkv, G, dq).transpose(1, 0, 2, 3, 4, 5)

    def one(qblk):
        s = jnp.einsum('bqhgd,bthd->bhgqt', qblk, k).astype(jnp.float32) * scale
        p = jax.nn.softmax(s, axis=-1).astype(v.dtype)
        return jnp.einsum('bhgqt,bthd->bqhgd', p, v)

    out = lax.map(one, qb)
    dv = v.shape[-1]
    return out.transpose(1, 0, 2, 3, 4, 5).reshape(B, S, Hkv * G * dv)


def gqa_mixer(h, hc, cos, sin, w_qkv, q_norm, k_norm, w_o, with_ctx_out):
    def project(t):
        B, T, _ = t.shape
        qkv = t @ w_qkv
        q = qkv[..., :GQA_HEADS * GQA_HEAD_DIM].reshape(B, T, GQA_HEADS, GQA_HEAD_DIM)
        k = qkv[..., GQA_HEADS * GQA_HEAD_DIM:(GQA_HEADS + GQA_KV_HEADS) * GQA_HEAD_DIM].reshape(B, T, GQA_KV_HEADS, GQA_HEAD_DIM)
        v = qkv[..., (GQA_HEADS + GQA_KV_HEADS) * GQA_HEAD_DIM:].reshape(B, T, GQA_KV_HEADS, GQA_HEAD_DIM)
        return rmsnorm(q, q_norm), rmsnorm(k, k_norm), v

    B, S, _ = h.shape
    q, k, v = project(h)
    qc, kc, vc = project(hc)
    q = apply_rope(q, cos, sin)
    k = apply_rope(k, cos, sin)
    scale = GQA_HEAD_DIM ** -0.5
    k_all = jnp.concatenate([kc, k], axis=1)
    v_all = jnp.concatenate([vc, v], axis=1)
    y = attend_blocks(q.reshape(B, S, GQA_KV_HEADS, GQA_GROUP, GQA_HEAD_DIM), k_all, v_all, scale) @ w_o
    yc = None
    if with_ctx_out:
        Bc, Tc = hc.shape[0], hc.shape[1]
        yc = attend_blocks(qc.reshape(Bc, Tc, GQA_KV_HEADS, GQA_GROUP, GQA_HEAD_DIM), kc, vc, scale) @ w_o
    return y, yc


def mla_mixer(h, hc, cos, sin, w_in, q_a_norm, q_b, kv_a_norm, kv_b, w_o, with_ctx_out):
    def project(t):
        B, T, _ = t.shape
        a = t @ w_in
        q_a = a[..., :MLA_Q_RANK]
        kv_a = a[..., MLA_Q_RANK:MLA_Q_RANK + MLA_KV_RANK]
        k_rope = a[..., MLA_Q_RANK + MLA_KV_RANK:][:, :, None, :]
        q = (rmsnorm(q_a, q_a_norm) @ q_b).reshape(B, T, MLA_HEADS, MLA_QK_DIM)
        kv = (rmsnorm(kv_a, kv_a_norm) @ kv_b).reshape(B, T, MLA_HEADS, MLA_NOPE_DIM + MLA_V_DIM)
        return q[..., :MLA_NOPE_DIM], q[..., MLA_NOPE_DIM:], kv[..., :MLA_NOPE_DIM], k_rope, kv[..., MLA_NOPE_DIM:]

    def assemble(q_nope, q_rope, k_nope, k_rope):
        q = jnp.concatenate([q_nope, q_rope], axis=-1)
        k = jnp.concatenate([k_nope, jnp.broadcast_to(k_rope, k_nope.shape[:3] + (MLA_ROPE_DIM,))], axis=-1)
        return q, k

    B, S, _ = h.shape
    q_nope, q_rope, k_nope, k_rope, v = project(h)
    q_rope = apply_rope(q_rope, cos, sin)
    k_rope = apply_rope(k_rope, cos, sin)
    q, k = assemble(q_nope, q_rope, k_nope, k_rope)
    qc_nope, qc_rope, kc_nope, kc_rope, vc = project(hc)
    qc, kc = assemble(qc_nope, qc_rope, kc_nope, kc_rope)
    scale = MLA_QK_DIM ** -0.5
    k_all = jnp.concatenate([kc, k], axis=1)
    v_all = jnp.concatenate([vc, v], axis=1)
    y = attend_blocks(q[:, :, :, None, :], k_all, v_all, scale) @ w_o
    yc = None
    if with_ctx_out:
        yc = attend_blocks(qc[:, :, :, None, :], kc, vc, scale) @ w_o
    return y, yc


def swiglu(h, w_gate_up, w_down):
    gu = h @ w_gate_up
    return (jax.nn.silu(gu[..., :D_FF]) * gu[..., D_FF:]) @ w_down


def setup_inputs(seed: int = 0) -> dict:
    key = jax.random.key(seed)
    ks = jax.random.split(key, 24)

    def nrm(k, shape, scale):
        return jax.random.normal(k, shape, jnp.float32) * scale

    D = D_MODEL
    return {
        "x": nrm(ks[0], (BATCH, SEQ, D), 1.0),
        "c": nrm(ks[1], (BATCH, D), 1.0),
        "ctx": nrm(ks[2], (BATCH, CTX_LEN, D), 1.0),
        "c_ctx": nrm(ks[3], (D,), 1.0),
        "norm1": 1.0 + nrm(ks[4], (DEPTH, D), 0.02),
        "norm2": 1.0 + nrm(ks[5], (DEPTH, D), 0.02),
        "w_mod": nrm(ks[6], (DEPTH, D, 6 * D), 0.5 * D ** -0.5),
        "b_mod": nrm(ks[7], (DEPTH, 6 * D), 0.02),
        "w_gate_up": nrm(ks[8], (DEPTH, D, 2 * D_FF), D ** -0.5),
        "w_down": nrm(ks[9], (DEPTH, D_FF, D), D_FF ** -0.5),
        "gqa_w_qkv": nrm(ks[10], (N_A, D, (GQA_HEADS + 2 * GQA_KV_HEADS) * GQA_HEAD_DIM), D ** -0.5),
        "gqa_q_norm": 1.0 + nrm(ks[11], (N_A, GQA_HEAD_DIM), 0.02),
        "gqa_k_norm": 1.0 + nrm(ks[12], (N_A, GQA_HEAD_DIM), 0.02),
        "gqa_w_o": nrm(ks[13], (N_A, GQA_HEADS * GQA_HEAD_DIM, D), (GQA_HEADS * GQA_HEAD_DIM) ** -0.5),
        "mla_w_in": nrm(ks[14], (N_B, D, MLA_Q_RANK + MLA_KV_RANK + MLA_ROPE_DIM), D ** -0.5),
        "mla_q_a_norm": 1.0 + nrm(ks[15], (N_B, MLA_Q_RANK), 0.02),
        "mla_q_b": nrm(ks[16], (N_B, MLA_Q_RANK, MLA_HEADS * MLA_QK_DIM), MLA_Q_RANK ** -0.5),
        "mla_kv_a_norm": 1.0 + nrm(ks[17], (N_B, MLA_KV_RANK), 0.02),
        "mla_kv_b": nrm(ks[18], (N_B, MLA_KV_RANK, MLA_HEADS * (MLA_NOPE_DIM + MLA_V_DIM)), MLA_KV_RANK ** -0.5),
        "mla_w_o": nrm(ks[19], (N_B, MLA_HEADS * MLA_V_DIM, D), (MLA_HEADS * MLA_V_DIM) ** -0.5),
        "final_norm": 1.0 + nrm(ks[20], (D,), 0.02),
    }


def reference(x, c, ctx, c_ctx, norm1, norm2, w_mod, b_mod, w_gate_up, w_down,
              gqa_w_qkv, gqa_q_norm, gqa_k_norm, gqa_w_o,
              mla_w_in, mla_q_a_norm, mla_q_b, mla_kv_a_norm, mla_kv_b, mla_w_o,
              final_norm):
    S = x.shape[1]
    rows = S // GRID_W
    cos_a, sin_a = axial_rope(rows, GQA_HEAD_DIM)
    cos_b, sin_b = axial_rope(rows, MLA_ROPE_DIM)
    sc = jax.nn.silu(c)
    scc = jax.nn.silu(c_ctx)
    xc = ctx
    for i in range(DEPTH):
        last = i == DEPTH - 1
        mod = sc @ w_mod[i] + b_mod[i]
        modc = scc @ w_mod[i] + b_mod[i]
        sh1, s1, g1, sh2, s2, g2 = [m[:, None, :] for m in jnp.split(mod, 6, axis=-1)]
        sh1c, s1c, g1c, sh2c, s2c, g2c = jnp.split(modc, 6, axis=-1)
        h = modulate(rmsnorm(x, norm1[i]), sh1, s1)
        hc = modulate(rmsnorm(xc, norm1[i]), sh1c, s1c)
        j = i // N_MIXERS
        if i % N_MIXERS == 0:
            y, yc = gqa_mixer(h, hc, cos_a, sin_a, gqa_w_qkv[j], gqa_q_norm[j], gqa_k_norm[j], gqa_w_o[j],
                              not last)
        else:
            y, yc = mla_mixer(h, hc, cos_b, sin_b, mla_w_in[j], mla_q_a_norm[j], mla_q_b[j],
                              mla_kv_a_norm[j], mla_kv_b[j], mla_w_o[j], not last)
        x = x + g1 * y
        x = x + g2 * swiglu(modulate(rmsnorm(x, norm2[i]), sh2, s2), w_gate_up[i], w_down[i])
        if not last:
            xc = xc + g1c * yc
            xc = xc + g2c * swiglu(modulate(rmsnorm(xc, norm2[i]), sh2c, s2c), w_gate_up[i], w_down[i])
    return rmsnorm(x, final_norm)
```

```python
import functools

import jax
import jax.numpy as jnp
from jax import lax
from jax.experimental import pallas as pl
from jax.experimental.pallas import tpu as pltpu

D_MODEL = 1024
BATCH = 8
SEQ = 4096
DEPTH = 2
GRID_W = 64
CTX_LEN = 256
ROPE_THETA = 10000.0
EPS = 1e-6

GQA_HEADS = 16
GQA_KV_HEADS = 4
GQA_HEAD_DIM = 64

MLA_HEADS = 16
MLA_NOPE_DIM = 64
MLA_ROPE_DIM = 32
MLA_V_DIM = 64
MLA_Q_RANK = 384
MLA_KV_RANK = 256
MLA_QK_DIM = MLA_NOPE_DIM + MLA_ROPE_DIM

D_FF = 2816

LANES = 128
T_ALL = CTX_LEN + SEQ
TM = 256
N_TILES = T_ALL // TM
MOD_ROWS = 16
CTX_ROW = BATCH
HEAD_PAIRS = 8
ATTN_SUB = 128
FF_CHUNK = 1408
W_IN_PAD = 768
VMEM_LIMIT = 56 * 1024 * 1024

_F32 = jnp.float32
_BF16 = jnp.bfloat16


def _resident(shape):
    zeros = (0,) * len(shape)
    return pl.BlockSpec(shape, lambda *_: zeros, pipeline_mode=pl.Buffered(1))


def _rms(x, gain):
    return x * lax.rsqrt(jnp.mean(x * x, axis=-1, keepdims=True) + EPS) * gain


def _mod_row_index(b, ti):
    return jnp.where(ti == 0, CTX_ROW, b)


def _mod_kernel(c_ref, w_ref, b_ref, o_ref):
    c = c_ref[...]
    sc = c * jax.nn.sigmoid(c)
    o_ref[0] = jnp.dot(sc, w_ref[0], preferred_element_type=_F32,
                       precision=lax.Precision.HIGHEST) + b_ref[0]


def _modulation(cvec, w_mod, b_mod):
    tn = 1536
    n = 6 * D_MODEL
    return pl.pallas_call(
        _mod_kernel,
        grid=(DEPTH, n // tn),
        in_specs=[
            pl.BlockSpec((MOD_ROWS, D_MODEL), lambda l, j: (0, 0)),
            pl.BlockSpec((1, D_MODEL, tn), lambda l, j: (l, 0, j)),
            pl.BlockSpec((1, 1, tn), lambda l, j: (l, 0, j)),
        ],
        out_specs=pl.BlockSpec((1, MOD_ROWS, tn), lambda l, j: (l, 0, j)),
        out_shape=jax.ShapeDtypeStruct((DEPTH, MOD_ROWS, n), _F32),
        compiler_params=pltpu.CompilerParams(vmem_limit_bytes=VMEM_LIMIT),
        name="modulation",
    )(cvec, w_mod, b_mod.reshape(DEPTH, 1, n))


def _rope(y, cos, sin_lo, sin_hi, half):
    return (y * cos + pltpu.roll(y, LANES - half, axis=1) * sin_lo
            + pltpu.roll(y, half, axis=1) * sin_hi)


def _gqa_proj_kernel(x_ref, mod_ref, n1_ref, w_ref, qn_ref, kn_ref, seg_ref,
                     cos_ref, slo_ref, shi_ref, q_ref, k_ref, v_ref):
    x = x_ref[0]
    mod = mod_ref[0]
    sh1 = mod[:, 0:D_MODEL]
    s1 = mod[:, D_MODEL:2 * D_MODEL]
    h = (_rms(x, n1_ref[...]) * (1.0 + s1) + sh1).astype(_BF16)
    qkv = jnp.dot(h, w_ref[...], preferred_element_type=_F32)
    cos = cos_ref[...]
    slo = slo_ref[...]
    shi = shi_ref[...]
    seg = seg_ref[...]
    low = lax.broadcasted_iota(jnp.int32, (1, LANES), 1) < GQA_HEAD_DIM

    def head_norm_rope(slab, gain):
        ms = jnp.dot(slab * slab, seg, preferred_element_type=_F32,
                     precision=lax.Precision.HIGHEST)
        y = slab * lax.rsqrt(ms + EPS) * gain
        return _rope(y, cos, slo, shi, GQA_HEAD_DIM // 2)

    n_q = GQA_HEADS * GQA_HEAD_DIM // LANES
    scale = GQA_HEAD_DIM ** -0.5
    for j in range(n_q):
        slab = qkv[:, j * LANES:(j + 1) * LANES]
        q_ref[0, :, j * LANES:(j + 1) * LANES] = (
            head_norm_rope(slab, qn_ref[...]) * scale).astype(_BF16)

    def store_duplicated(dst_ref, pair, y):
        swapped = pltpu.roll(y, GQA_HEAD_DIM, axis=1)
        dst_ref[0, 2 * pair] = jnp.where(low, y, swapped).astype(_BF16)
        dst_ref[0, 2 * pair + 1] = jnp.where(low, swapped, y).astype(_BF16)

    n_kv = GQA_KV_HEADS * GQA_HEAD_DIM // LANES
    for j in range(n_kv):
        kslab = qkv[:, (n_q + j) * LANES:(n_q + j + 1) * LANES]
        store_duplicated(k_ref, j, head_norm_rope(kslab, kn_ref[...]))
        vslab = qkv[:, (n_q + n_kv + j) * LANES:(n_q + n_kv + j + 1) * LANES]
        store_duplicated(v_ref, j, vslab)


def _gqa_proj(x_all, mod, norm1, w_qkv, q_norm, k_norm, seg, cos, slo, shi):
    n_out = w_qkv.shape[1]
    tok = lambda b, ti: (b, ti, 0)
    tab = pl.BlockSpec((TM, LANES), lambda b, ti: (ti, 0))
    kv_spec = pl.BlockSpec((1, GQA_KV_HEADS, TM, LANES), lambda b, ti: (b, 0, ti, 0))
    return pl.pallas_call(
        _gqa_proj_kernel,
        grid=(BATCH, N_TILES),
        in_specs=[
            pl.BlockSpec((1, TM, D_MODEL), tok),
            pl.BlockSpec((1, 1, 6 * D_MODEL), lambda b, ti: (_mod_row_index(b, ti), 0, 0)),
            _resident((1, D_MODEL)),
            _resident((D_MODEL, n_out)),
            _resident((1, LANES)),
            _resident((1, LANES)),
            _resident((LANES, LANES)),
            tab, tab, tab,
        ],
        out_specs=[
            pl.BlockSpec((1, TM, GQA_HEADS * GQA_HEAD_DIM), tok),
            kv_spec, kv_spec,
        ],
        out_shape=[
            jax.ShapeDtypeStruct((BATCH, T_ALL, GQA_HEADS * GQA_HEAD_DIM), _BF16),
            jax.ShapeDtypeStruct((BATCH, GQA_KV_HEADS, T_ALL, LANES), _BF16),
            jax.ShapeDtypeStruct((BATCH, GQA_KV_HEADS, T_ALL, LANES), _BF16),
        ],
        compiler_params=pltpu.CompilerParams(
            dimension_semantics=("parallel", "parallel"), vmem_limit_bytes=VMEM_LIMIT),
        name="gqa_proj",
    )(x_all, mod, norm1, w_qkv, q_norm, k_norm, seg, cos, slo, shi)


def _mla_proj_kernel(x_ref, mod_ref, n1_ref, win_ref, qan_ref, kvan_ref, qb_ref, kvb_ref,
                     cos_ref, slo_ref, shi_ref, q_ref, k_ref, v_ref):
    x = x_ref[0]
    mod = mod_ref[0]
    sh1 = mod[:, 0:D_MODEL]
    s1 = mod[:, D_MODEL:2 * D_MODEL]
    h = (_rms(x, n1_ref[...]) * (1.0 + s1) + sh1).astype(_BF16)
    a = jnp.dot(h, win_ref[...], preferred_element_type=_F32)
    cos = cos_ref[...]
    slo = slo_ref[...]
    shi = shi_ref[...]
    half = MLA_ROPE_DIM // 2
    scale = MLA_QK_DIM ** -0.5

    qn = _rms(a[:, :MLA_Q_RANK], qan_ref[...]).astype(_BF16)
    qf = jnp.dot(qn, qb_ref[...], preferred_element_type=_F32)
    for hd in range(MLA_HEADS):
        slab = qf[:, hd * LANES:(hd + 1) * LANES]
        q_ref[0, :, hd * LANES:(hd + 1) * LANES] = (
            _rope(slab, cos, slo, shi, half) * scale).astype(_BF16)

    kv_lo = MLA_Q_RANK
    kv_hi = MLA_Q_RANK + MLA_KV_RANK
    kvn = _rms(a[:, kv_lo:kv_hi], kvan_ref[...]).astype(_BF16)
    kvf = jnp.dot(kvn, kvb_ref[...], preferred_element_type=_F32)
    kr = pltpu.roll(a[:, kv_hi:kv_hi + LANES], MLA_NOPE_DIM, axis=1)
    kr = _rope(kr, cos, slo, shi, half)
    for hd in range(MLA_HEADS):
        k_ref[0, :, hd * LANES:(hd + 1) * LANES] = (
            kvf[:, hd * LANES:(hd + 1) * LANES] + kr).astype(_BF16)
    v_ref[0] = kvf[:, MLA_HEADS * LANES:].astype(_BF16)


def _mla_proj(x_all, mod, norm1, w_in, q_a_norm, kv_a_norm, q_b, kv_b, cos, slo, shi):
    tok = lambda b, ti: (b, ti, 0)
    tab = pl.BlockSpec((TM, LANES), lambda b, ti: (ti, 0))
    qk_cols = MLA_HEADS * LANES
    v_cols = MLA_HEADS * MLA_V_DIM
    return pl.pallas_call(
        _mla_proj_kernel,
        grid=(BATCH, N_TILES),
        in_specs=[
            pl.BlockSpec((1, TM, D_MODEL), tok),
            pl.BlockSpec((1, 1, 6 * D_MODEL), lambda b, ti: (_mod_row_index(b, ti), 0, 0)),
            _resident((1, D_MODEL)),
            _resident((D_MODEL, W_IN_PAD)),
            _resident((1, MLA_Q_RANK)),
            _resident((1, MLA_KV_RANK)),
            _resident((MLA_Q_RANK, qk_cols)),
            _resident((MLA_KV_RANK, qk_cols + v_cols)),
            tab, tab, tab,
        ],
        out_specs=[
            pl.BlockSpec((1, TM, qk_cols), tok),
            pl.BlockSpec((1, TM, qk_cols), tok),
            pl.BlockSpec((1, TM, v_cols), tok),
        ],
        out_shape=[
            jax.ShapeDtypeStruct((BATCH, T_ALL, qk_cols), _BF16),
            jax.ShapeDtypeStruct((BATCH, T_ALL, qk_cols), _BF16),
            jax.ShapeDtypeStruct((BATCH, T_ALL, v_cols), _BF16),
        ],
        compiler_params=pltpu.CompilerParams(
            dimension_semantics=("parallel", "parallel"), vmem_limit_bytes=VMEM_LIMIT),
        name="mla_proj",
    )(x_all, mod, norm1, w_in, q_a_norm, kv_a_norm, q_b, kv_b, cos, slo, shi)


def _attn_kernel(q_ref, k_ref, v_ref, o_ref, *, shared_kv, first_tile):
    low = lax.broadcasted_iota(jnp.int32, (1, LANES), 1) < LANES // 2
    nt = (((1,), (1,)), ((), ()))

    def attend(klen):
        if shared_kv:
            keys = [k_ref[0, 0, :klen, :]] * 2
            vals = v_ref[0, 0, :klen, :]
        else:
            keys = [k_ref[0, :klen, :LANES], k_ref[0, :klen, LANES:]]
            vals = v_ref[0, :klen, :]
        for r in range(TM // ATTN_SUB):
            rows = slice(r * ATTN_SUB, (r + 1) * ATTN_SUB)
            res = []
            for hh in range(2):
                if shared_kv:
                    qs = q_ref[0, rows, :]
                    qs = jnp.where(low if hh == 0 else jnp.logical_not(low), qs,
                                   jnp.zeros_like(qs))
                else:
                    qs = q_ref[0, rows, hh * LANES:(hh + 1) * LANES]
                s = lax.dot_general(qs, keys[hh], nt, preferred_element_type=_F32)
                m = jnp.max(s, axis=-1, keepdims=True)
                p = jnp.exp(s - m)
                l = jnp.sum(p, axis=-1, keepdims=True)
                o = jnp.dot(p.astype(_BF16), vals, preferred_element_type=_F32)
                res.append(o / l)
            o_ref[0, rows, :] = jnp.where(low, res[0], res[1]).astype(_BF16)

    if first_tile == 0:
        qi = pl.program_id(2)
        pl.when(qi == 0)(lambda: attend(CTX_LEN))
        pl.when(qi != 0)(lambda: attend(T_ALL))
    else:
        attend(T_ALL)


def _attention(q, k, v, *, shared_kv, first_tile):
    nq = N_TILES - first_tile
    q_cols = LANES if shared_kv else 2 * LANES
    q_spec = pl.BlockSpec((1, TM, q_cols), lambda b, hp, qi: (b, qi + first_tile, hp))
    if shared_kv:
        k_spec = pl.BlockSpec((1, 1, T_ALL, LANES), lambda b, hp, qi: (b, hp // 2, 0, 0))
        v_spec = k_spec
    else:
        k_spec = pl.BlockSpec((1, T_ALL, 2 * LANES), lambda b, hp, qi: (b, 0, hp))
        v_spec = pl.BlockSpec((1, T_ALL, LANES), lambda b, hp, qi: (b, 0, hp))
    return pl.pallas_call(
        functools.partial(_attn_kernel, shared_kv=shared_kv, first_tile=first_tile),
        grid=(BATCH, HEAD_PAIRS, nq),
        in_specs=[q_spec, k_spec, v_spec],
        out_specs=pl.BlockSpec((1, TM, LANES), lambda b, hp, qi: (b, qi, hp)),
        out_shape=jax.ShapeDtypeStruct((BATCH, nq * TM, HEAD_PAIRS * LANES), _BF16),
        compiler_params=pltpu.CompilerParams(
            dimension_semantics=("parallel", "parallel", "arbitrary"),
            vmem_limit_bytes=VMEM_LIMIT),
        name="attention_gqa" if shared_kv else "attention_mla",
    )(q, k, v)


def _post_kernel(x_ref, o_ref, mod_ref, n2_ref, wo_ref, wgu_ref, wd_ref, fn_ref, out_ref,
                 *, last):
    mod = mod_ref[0]
    g1 = mod[:, 2 * D_MODEL:3 * D_MODEL]
    sh2 = mod[:, 3 * D_MODEL:4 * D_MODEL]
    s2 = mod[:, 4 * D_MODEL:5 * D_MODEL]
    g2 = mod[:, 5 * D_MODEL:6 * D_MODEL]
    x1 = x_ref[0] + g1 * jnp.dot(o_ref[0], wo_ref[...], preferred_element_type=_F32)
    h2 = (_rms(x1, n2_ref[...]) * (1.0 + s2) + sh2).astype(_BF16)
    y = None
    for c in range(D_FF // FF_CHUNK):
        lo = c * FF_CHUNK
        gate = jnp.dot(h2, wgu_ref[:, lo:lo + FF_CHUNK], preferred_element_type=_F32)
        up = jnp.dot(h2, wgu_ref[:, D_FF + lo:D_FF + lo + FF_CHUNK],
                     preferred_element_type=_F32)
        act = (gate * jax.nn.sigmoid(gate) * up).astype(_BF16)
        part = jnp.dot(act, wd_ref[lo:lo + FF_CHUNK, :], preferred_element_type=_F32)
        y = part if y is None else y + part
    x2 = x1 + g2 * y
    out_ref[0] = _rms(x2, fn_ref[...]) if last else x2


def _post(x_all, o, mod, norm2, w_o, w_gu, w_down, final_norm, *, last):
    first_tile = 1 if last else 0
    nt = N_TILES - first_tile
    tok_in = lambda b, ti: (b, ti + first_tile, 0)
    if last:
        mod_idx = lambda b, ti: (b, 0, 0)
    else:
        mod_idx = lambda b, ti: (_mod_row_index(b, ti), 0, 0)
    return pl.pallas_call(
        functools.partial(_post_kernel, last=last),
        grid=(BATCH, nt),
        in_specs=[
            pl.BlockSpec((1, TM, D_MODEL), tok_in),
            pl.BlockSpec((1, TM, D_MODEL), lambda b, ti: (b, ti, 0)),
            pl.BlockSpec((1, 1, 6 * D_MODEL), mod_idx),
            _resident((1, D_MODEL)),
            _resident((D_MODEL, D_MODEL)),
            _resident((D_MODEL, 2 * D_FF)),
            _resident((D_FF, D_MODEL)),
            _resident((1, D_MODEL)),
        ],
        out_specs=pl.BlockSpec((1, TM, D_MODEL), lambda b, ti: (b, ti, 0)),
        out_shape=jax.ShapeDtypeStruct((BATCH, nt * TM, D_MODEL), _F32),
        compiler_params=pltpu.CompilerParams(
            dimension_semantics=("parallel", "parallel"), vmem_limit_bytes=VMEM_LIMIT),
        name="post_last" if last else "post",
    )(x_all, o, mod, norm2, w_o, w_gu, w_down, final_norm)


def _axial_angles(rot_dim):
    n = rot_dim // 4
    inv = ROPE_THETA ** (-jnp.arange(n, dtype=_F32) / n)
    rows = SEQ // GRID_W
    row = jnp.repeat(jnp.arange(rows, dtype=_F32), GRID_W)
    col = jnp.tile(jnp.arange(GRID_W, dtype=_F32), rows)
    return jnp.concatenate([row[:, None] * inv, col[:, None] * inv], axis=-1)


def _with_context_rows(cos, sin_lo, sin_hi):
    pad = lambda t, v: jnp.concatenate([jnp.full((CTX_LEN, LANES), v, _F32), t], axis=0)
    return pad(cos, 1.0), pad(sin_lo, 0.0), pad(sin_hi, 0.0)


def _gqa_tables():
    ang = _axial_angles(GQA_HEAD_DIM)
    cos, sin = jnp.cos(ang), jnp.sin(ang)
    zero = jnp.zeros_like(sin)
    reps = LANES // GQA_HEAD_DIM
    cos_t = jnp.tile(jnp.concatenate([cos, cos], -1), (1, reps))
    sin_lo = jnp.tile(jnp.concatenate([-sin, zero], -1), (1, reps))
    sin_hi = jnp.tile(jnp.concatenate([zero, sin], -1), (1, reps))
    return _with_context_rows(cos_t, sin_lo, sin_hi)


def _mla_tables():
    ang = _axial_angles(MLA_ROPE_DIM)
    cos, sin = jnp.cos(ang), jnp.sin(ang)
    zero = jnp.zeros_like(sin)
    ones_nope = jnp.ones((SEQ, MLA_NOPE_DIM), _F32)
    zeros_nope = jnp.zeros((SEQ, MLA_NOPE_DIM), _F32)
    tail = LANES - MLA_QK_DIM
    cos_t = jnp.concatenate([ones_nope, cos, cos, jnp.ones((SEQ, tail), _F32)], -1)
    sin_lo = jnp.concatenate([zeros_nope, -sin, zero, jnp.zeros((SEQ, tail), _F32)], -1)
    sin_hi = jnp.concatenate([zeros_nope, zero, sin, jnp.zeros((SEQ, tail), _F32)], -1)
    return _with_context_rows(cos_t, sin_lo, sin_hi)


def _mla_weight_layout(w_in, q_b, kv_b):
    w_in_p = jnp.pad(w_in, ((0, 0), (0, W_IN_PAD - w_in.shape[1])))
    qb = q_b.reshape(MLA_Q_RANK, MLA_HEADS, MLA_QK_DIM)
    qb = jnp.pad(qb, ((0, 0), (0, 0), (0, LANES - MLA_QK_DIM)))
    qb = qb.reshape(MLA_Q_RANK, MLA_HEADS * LANES)
    kvb = kv_b.reshape(MLA_KV_RANK, MLA_HEADS, MLA_NOPE_DIM + MLA_V_DIM)
    kb = jnp.pad(kvb[:, :, :MLA_NOPE_DIM], ((0, 0), (0, 0), (0, LANES - MLA_NOPE_DIM)))
    kb = kb.reshape(MLA_KV_RANK, MLA_HEADS * LANES)
    vb = kvb[:, :, MLA_NOPE_DIM:].reshape(MLA_KV_RANK, MLA_HEADS * MLA_V_DIM)
    return (w_in_p.astype(_BF16), qb.astype(_BF16),
            jnp.concatenate([kb, vb], axis=-1).astype(_BF16))


def kernel(x, c, ctx, c_ctx, norm1, norm2, w_mod, b_mod, w_gate_up, w_down, gqa_w_qkv,
           gqa_q_norm, gqa_k_norm, gqa_w_o, mla_w_in, mla_q_a_norm, mla_q_b, mla_kv_a_norm,
           mla_kv_b, mla_w_o, final_norm):
    assert x.shape == (BATCH, SEQ, D_MODEL) and ctx.shape == (BATCH, CTX_LEN, D_MODEL)

    cvec = jnp.zeros((MOD_ROWS, D_MODEL), _F32).at[:BATCH].set(c).at[CTX_ROW].set(c_ctx)
    mod = _modulation(cvec, w_mod, b_mod).reshape(DEPTH, MOD_ROWS, 1, 6 * D_MODEL)

    x_all = jnp.concatenate([ctx, x], axis=1)
    fn = final_norm.reshape(1, D_MODEL)
    w_gu = w_gate_up.astype(_BF16)
    w_dn = w_down.astype(_BF16)

    seg = jnp.kron(jnp.eye(LANES // GQA_HEAD_DIM, dtype=_F32),
                   jnp.full((GQA_HEAD_DIM, GQA_HEAD_DIM), 1.0 / GQA_HEAD_DIM, _F32))
    reps = LANES // GQA_HEAD_DIM
    q, k, v = _gqa_proj(
        x_all, mod[0], norm1[0].reshape(1, D_MODEL), gqa_w_qkv[0].astype(_BF16),
        jnp.tile(gqa_q_norm[0], reps).reshape(1, LANES),
        jnp.tile(gqa_k_norm[0], reps).reshape(1, LANES), seg, *_gqa_tables())
    o = _attention(q, k, v, shared_kv=True, first_tile=0)
    x_all = _post(x_all, o, mod[0], norm2[0].reshape(1, D_MODEL), gqa_w_o[0].astype(_BF16),
                  w_gu[0], w_dn[0], fn, last=False)

    w_in, q_b, kv_b = _mla_weight_layout(mla_w_in[0], mla_q_b[0], mla_kv_b[0])
    q, k, v = _mla_proj(
        x_all, mod[1], norm1[1].reshape(1, D_MODEL), w_in,
        mla_q_a_norm[0].reshape(1, MLA_Q_RANK), mla_kv_a_norm[0].reshape(1, MLA_KV_RANK),
        q_b, kv_b, *_mla_tables())
    o = _attention(q, k, v, shared_kv=False, first_tile=1)
    return _post(x_all, o, mod[1], norm2[1].reshape(1, D_MODEL), mla_w_o[0].astype(_BF16),
                 w_gu[1], w_dn[1], fn, last=True)
```

```python
import functools

import jax
import jax.numpy as jnp
from jax import lax
from jax.experimental import pallas as pl
from jax.experimental.pallas import tpu as pltpu

D_MODEL = 1024
BATCH = 8
SEQ = 4096
DEPTH = 2
GRID_W = 64
CTX_LEN = 256
ROPE_THETA = 10000.0
EPS = 1e-6

GQA_HEADS = 16
GQA_KV_HEADS = 4
GQA_HEAD_DIM = 64

MLA_HEADS = 16
MLA_NOPE_DIM = 64
MLA_ROPE_DIM = 32
MLA_V_DIM = 64
MLA_Q_RANK = 384
MLA_KV_RANK = 256
MLA_QK_DIM = MLA_NOPE_DIM + MLA_ROPE_DIM

D_FF = 2816

LANES = 128
T_ALL = CTX_LEN + SEQ
TM = 256
N_TILES = T_ALL // TM
MOD_ROWS = 16
CTX_ROW = BATCH
HEAD_PAIRS = 8
KEY_TILE = 256
BF16_SUBLANES = 16
FF_CHUNK = 1408
W_IN_PAD = 768
VMEM_LIMIT = 56 * 1024 * 1024
LOG2_E = 1.4426950408889634

_F32 = jnp.float32
_BF16 = jnp.bfloat16


def _resident(shape):
    zeros = (0,) * len(shape)
    return pl.BlockSpec(shape, lambda *_: zeros, pipeline_mode=pl.Buffered(1))


def _rms(x, gain):
    return x * lax.rsqrt(jnp.mean(x * x, axis=-1, keepdims=True) + EPS) * gain


def _mod_row_index(b, ti):
    return jnp.where(ti == 0, CTX_ROW, b)


def _mod_kernel(c_ref, w_ref, b_ref, o_ref):
    c = c_ref[...]
    sc = c * jax.nn.sigmoid(c)
    o_ref[0] = jnp.dot(sc, w_ref[0], preferred_element_type=_F32,
                       precision=lax.Precision.HIGHEST) + b_ref[0]


def _modulation(cvec, w_mod, b_mod):
    tn = 1536
    n = 6 * D_MODEL
    return pl.pallas_call(
        _mod_kernel,
        grid=(DEPTH, n // tn),
        in_specs=[
            pl.BlockSpec((MOD_ROWS, D_MODEL), lambda l, j: (0, 0)),
            pl.BlockSpec((1, D_MODEL, tn), lambda l, j: (l, 0, j)),
            pl.BlockSpec((1, 1, tn), lambda l, j: (l, 0, j)),
        ],
        out_specs=pl.BlockSpec((1, MOD_ROWS, tn), lambda l, j: (l, 0, j)),
        out_shape=jax.ShapeDtypeStruct((DEPTH, MOD_ROWS, n), _F32),
        compiler_params=pltpu.CompilerParams(vmem_limit_bytes=VMEM_LIMIT),
        name="modulation",
    )(cvec, w_mod, b_mod.reshape(DEPTH, 1, n))


def _rope(y, cos, sin_lo, sin_hi, half):
    return (y * cos + pltpu.roll(y, LANES - half, axis=1) * sin_lo
            + pltpu.roll(y, half, axis=1) * sin_hi)


def _gqa_proj_kernel(x_ref, mod_ref, n1_ref, w_ref, qn_ref, kn_ref, seg_ref,
                     cos_ref, slo_ref, shi_ref, q_ref, k_ref, vt_ref):
    x = x_ref[0]
    mod = mod_ref[0]
    sh1 = mod[:, 0:D_MODEL]
    s1 = mod[:, D_MODEL:2 * D_MODEL]
    h = (_rms(x, n1_ref[...]) * (1.0 + s1) + sh1).astype(_BF16)
    qkv = jnp.dot(h, w_ref[...], preferred_element_type=_F32)
    cos = cos_ref[...]
    slo = slo_ref[...]
    shi = shi_ref[...]
    seg = seg_ref[...]
    low = lax.broadcasted_iota(jnp.int32, (1, LANES), 1) < GQA_HEAD_DIM

    def head_norm_rope(slab, gain):
        ms = jnp.dot(slab * slab, seg, preferred_element_type=_F32,
                     precision=lax.Precision.HIGHEST)
        y = slab * lax.rsqrt(ms + EPS) * gain
        return _rope(y, cos, slo, shi, GQA_HEAD_DIM // 2)

    n_q = GQA_HEADS * GQA_HEAD_DIM // LANES
    scale = GQA_HEAD_DIM ** -0.5 * LOG2_E
    for j in range(n_q):
        slab = qkv[:, j * LANES:(j + 1) * LANES]
        q_ref[0, :, j * LANES:(j + 1) * LANES] = (
            head_norm_rope(slab, qn_ref[...]) * scale).astype(_BF16)

    n_kv = GQA_KV_HEADS * GQA_HEAD_DIM // LANES
    for j in range(n_kv):
        kslab = qkv[:, (n_q + j) * LANES:(n_q + j + 1) * LANES]
        y = head_norm_rope(kslab, kn_ref[...])
        swapped = pltpu.roll(y, GQA_HEAD_DIM, axis=1)
        k_ref[0, 2 * j] = jnp.where(low, y, swapped).astype(_BF16)
        k_ref[0, 2 * j + 1] = jnp.where(low, swapped, y).astype(_BF16)
        vslab = qkv[:, (n_q + n_kv + j) * LANES:(n_q + n_kv + j + 1) * LANES]
        vt_ref[0, j * LANES:(j + 1) * LANES, :] = vslab.T.astype(_BF16)


def _gqa_proj(x_all, mod, norm1, w_qkv, q_norm, k_norm, seg, cos, slo, shi):
    n_out = w_qkv.shape[1]
    tok = lambda b, ti: (b, ti, 0)
    tab = pl.BlockSpec((TM, LANES), lambda b, ti: (ti, 0))
    kv_spec = pl.BlockSpec((1, GQA_KV_HEADS, TM, LANES), lambda b, ti: (b, 0, ti, 0))
    kv_cols = GQA_KV_HEADS * GQA_HEAD_DIM
    return pl.pallas_call(
        _gqa_proj_kernel,
        grid=(BATCH, N_TILES),
        in_specs=[
            pl.BlockSpec((1, TM, D_MODEL), tok),
            pl.BlockSpec((1, 1, 6 * D_MODEL), lambda b, ti: (_mod_row_index(b, ti), 0, 0)),
            _resident((1, D_MODEL)),
            _resident((D_MODEL, n_out)),
            _resident((1, LANES)),
            _resident((1, LANES)),
            _resident((LANES, LANES)),
            tab, tab, tab,
        ],
        out_specs=[
            pl.BlockSpec((1, TM, GQA_HEADS * GQA_HEAD_DIM), tok),
            kv_spec,
            pl.BlockSpec((1, kv_cols, TM), lambda b, ti: (b, 0, ti)),
        ],
        out_shape=[
            jax.ShapeDtypeStruct((BATCH, T_ALL, GQA_HEADS * GQA_HEAD_DIM), _BF16),
            jax.ShapeDtypeStruct((BATCH, GQA_KV_HEADS, T_ALL, LANES), _BF16),
            jax.ShapeDtypeStruct((BATCH, kv_cols, T_ALL), _BF16),
        ],
        compiler_params=pltpu.CompilerParams(
            dimension_semantics=("parallel", "parallel"), vmem_limit_bytes=VMEM_LIMIT),
        name="gqa_proj",
    )(x_all, mod, norm1, w_qkv, q_norm, k_norm, seg, cos, slo, shi)


def _mla_proj_kernel(x_ref, mod_ref, n1_ref, win_ref, qan_ref, kvan_ref, qb_ref, kvb_ref,
                     cos_ref, slo_ref, shi_ref, q_ref, k_ref, vt_ref):
    x = x_ref[0]
    mod = mod_ref[0]
    sh1 = mod[:, 0:D_MODEL]
    s1 = mod[:, D_MODEL:2 * D_MODEL]
    h = (_rms(x, n1_ref[...]) * (1.0 + s1) + sh1).astype(_BF16)
    a = jnp.dot(h, win_ref[...], preferred_element_type=_F32)
    cos = cos_ref[...]
    slo = slo_ref[...]
    shi = shi_ref[...]
    half = MLA_ROPE_DIM // 2
    scale = MLA_QK_DIM ** -0.5 * LOG2_E

    qn = _rms(a[:, :MLA_Q_RANK], qan_ref[...]).astype(_BF16)
    qf = jnp.dot(qn, qb_ref[...], preferred_element_type=_F32)
    for hd in range(MLA_HEADS):
        slab = qf[:, hd * LANES:(hd + 1) * LANES]
        q_ref[0, :, hd * LANES:(hd + 1) * LANES] = (
            _rope(slab, cos, slo, shi, half) * scale).astype(_BF16)

    kv_lo = MLA_Q_RANK
    kv_hi = MLA_Q_RANK + MLA_KV_RANK
    kvn = _rms(a[:, kv_lo:kv_hi], kvan_ref[...]).astype(_BF16)
    kvf = jnp.dot(kvn, kvb_ref[...], preferred_element_type=_F32)
    kr = pltpu.roll(a[:, kv_hi:kv_hi + LANES], MLA_NOPE_DIM, axis=1)
    kr = _rope(kr, cos, slo, shi, half)
    for hd in range(MLA_HEADS):
        k_ref[0, :, hd * LANES:(hd + 1) * LANES] = (
            kvf[:, hd * LANES:(hd + 1) * LANES] + kr).astype(_BF16)
    v_lo = MLA_HEADS * LANES
    for j in range(MLA_HEADS * MLA_V_DIM // LANES):
        vslab = kvf[:, v_lo + j * LANES:v_lo + (j + 1) * LANES]
        vt_ref[0, j * LANES:(j + 1) * LANES, :] = vslab.T.astype(_BF16)


def _mla_proj(x_all, mod, norm1, w_in, q_a_norm, kv_a_norm, q_b, kv_b, cos, slo, shi):
    tok = lambda b, ti: (b, ti, 0)
    tab = pl.BlockSpec((TM, LANES), lambda b, ti: (ti, 0))
    qk_cols = MLA_HEADS * LANES
    v_cols = MLA_HEADS * MLA_V_DIM
    return pl.pallas_call(
        _mla_proj_kernel,
        grid=(BATCH, N_TILES),
        in_specs=[
            pl.BlockSpec((1, TM, D_MODEL), tok),
            pl.BlockSpec((1, 1, 6 * D_MODEL), lambda b, ti: (_mod_row_index(b, ti), 0, 0)),
            _resident((1, D_MODEL)),
            _resident((D_MODEL, W_IN_PAD)),
            _resident((1, MLA_Q_RANK)),
            _resident((1, MLA_KV_RANK)),
            _resident((MLA_Q_RANK, qk_cols)),
            _resident((MLA_KV_RANK, qk_cols + v_cols)),
            tab, tab, tab,
        ],
        out_specs=[
            pl.BlockSpec((1, TM, qk_cols), tok),
            pl.BlockSpec((1, TM, qk_cols), tok),
            pl.BlockSpec((1, v_cols, TM), lambda b, ti: (b, 0, ti)),
        ],
        out_shape=[
            jax.ShapeDtypeStruct((BATCH, T_ALL, qk_cols), _BF16),
            jax.ShapeDtypeStruct((BATCH, T_ALL, qk_cols), _BF16),
            jax.ShapeDtypeStruct((BATCH, v_cols, T_ALL), _BF16),
        ],
        compiler_params=pltpu.CompilerParams(
            dimension_semantics=("parallel", "parallel"), vmem_limit_bytes=VMEM_LIMIT),
        name="mla_proj",
    )(x_all, mod, norm1, w_in, q_a_norm, kv_a_norm, q_b, kv_b, cos, slo, shi)


def _attn_kernel(q_ref, k_ref, vt_ref, o_ref, st0_ref, st1_ref, *, shared_kv, n_blocks, first_row):
    klen = st0_ref.shape[1]
    n_kt = klen // KEY_TILE
    low = lax.broadcasted_iota(jnp.int32, (1, LANES), 1) < LANES // 2
    nt = (((1,), (1,)), ((), ()))
    hd = LANES // 2
    ones = jnp.ones((BF16_SUBLANES, KEY_TILE), _BF16)

    def block_rows(r, base):
        start = base + r * TM
        return pl.ds(start if isinstance(start, int) else pl.multiple_of(start, TM), TM)

    def query_block(r, hh):
        rows = block_rows(r, first_row)
        if not shared_kv:
            return q_ref[0, rows, hh * LANES:(hh + 1) * LANES]
        qs = q_ref[0, rows, :]
        return jnp.where(low if hh == 0 else jnp.logical_not(low), qs, jnp.zeros_like(qs))

    def key_tile(hh, rows):
        if shared_kv:
            return k_ref[0, 0, rows, :]
        return k_ref[0, rows, hh * LANES:(hh + 1) * LANES]

    def value_tile(hh, cols):
        if shared_kv:
            vt = vt_ref[0, :, cols]
        else:
            vt = vt_ref[0, hh * hd:(hh + 1) * hd, cols]
        return jnp.concatenate([vt, ones], axis=0)

    def tiles():
        for kt in range(n_kt):
            for hh in range(2):
                yield hh, slice(kt * KEY_TILE, (kt + 1) * KEY_TILE)

    def score_tile(hh, rows, qs, st_ref, m8):
        s_t = lax.dot_general(key_tile(hh, rows), qs[hh], nt, preferred_element_type=_F32)
        st_ref[hh, rows, :] = s_t
        t8 = jnp.max(s_t.reshape(KEY_TILE // 8, 8, TM), axis=0)
        m8[hh] = t8 if m8[hh] is None else jnp.maximum(m8[hh], t8)

    def column_max(m8):
        return tuple(jnp.broadcast_to(jnp.max(m, axis=0, keepdims=True), (8, TM)) for m in m8)

    def value_tile_step(hh, rows, st_ref, maxes, acc):
        d = st_ref[hh, rows, :].reshape(KEY_TILE // 8, 8, TM) - maxes[hh]
        p_t = jnp.exp2(d.reshape(KEY_TILE, TM).astype(_BF16))
        part = jnp.dot(value_tile(hh, rows), p_t, preferred_element_type=_F32)
        acc[hh] = part if acc[hh] is None else acc[hh] + part

    def write_block(r, acc):
        halves = [a[:hd] * (1.0 / a[hd:hd + 1]) for a in acc]
        o_ref[0, block_rows(r, 0), :] = jnp.concatenate(halves, axis=0).T.astype(_BF16)

    def pass1(r, st_ref):
        qs = [query_block(r, hh) for hh in range(2)]
        m8 = [None, None]
        for hh, rows in tiles():
            score_tile(hh, rows, qs, st_ref, m8)
        return column_max(m8)

    def pass2(r, st_ref, maxes):
        acc = [None, None]
        for hh, rows in tiles():
            value_tile_step(hh, rows, st_ref, maxes, acc)
        write_block(r, acc)

    def overlapped(r, st_next, st_cur, maxes):
        qs = [query_block(r + 1, hh) for hh in range(2)]
        m8 = [None, None]
        acc = [None, None]
        for hh, rows in tiles():
            score_tile(hh, rows, qs, st_next, m8)
            value_tile_step(hh, rows, st_cur, maxes, acc)
        write_block(r, acc)
        return column_max(m8)

    maxes = pass1(0, st0_ref)
    if n_blocks == 1:
        pass2(0, st0_ref, maxes)
        return
    assert n_blocks % 2 == 0

    def body(j, maxes):
        r = 2 * j
        return overlapped(r + 1, st0_ref, st1_ref, overlapped(r, st1_ref, st0_ref, maxes))

    maxes = lax.fori_loop(0, n_blocks // 2 - 1, body, maxes)
    maxes = overlapped(n_blocks - 2, st1_ref, st0_ref, maxes)
    pass2(n_blocks - 1, st1_ref, maxes)


def _attention(q, k, vt, *, shared_kv, context):
    klen = CTX_LEN if context else T_ALL
    q_rows = TM if context else T_ALL
    o_rows = TM if context else SEQ
    q_cols = LANES if shared_kv else 2 * LANES
    if shared_kv:
        k_spec = pl.BlockSpec((1, 1, klen, LANES), lambda b, hp: (b, hp // 2, 0, 0))
        vt_spec = pl.BlockSpec((1, GQA_HEAD_DIM, klen), lambda b, hp: (b, hp // 2, 0))
    else:
        k_spec = pl.BlockSpec((1, klen, 2 * LANES), lambda b, hp: (b, 0, hp))
        vt_spec = pl.BlockSpec((1, LANES, klen), lambda b, hp: (b, hp, 0))
    name = ("attention_gqa" if shared_kv else "attention_mla") + ("_ctx" if context else "")
    return pl.pallas_call(
        functools.partial(_attn_kernel, shared_kv=shared_kv, n_blocks=o_rows // TM,
                          first_row=q_rows - o_rows),
        grid=(BATCH, HEAD_PAIRS),
        in_specs=[pl.BlockSpec((1, q_rows, q_cols), lambda b, hp: (b, 0, hp)), k_spec, vt_spec],
        out_specs=pl.BlockSpec((1, o_rows, LANES), lambda b, hp: (b, 0, hp)),
        out_shape=jax.ShapeDtypeStruct((BATCH, o_rows, HEAD_PAIRS * LANES), _BF16),
        scratch_shapes=[pltpu.VMEM((2, klen, TM), _F32)] * 2,
        compiler_params=pltpu.CompilerParams(
            dimension_semantics=("parallel", "parallel"), vmem_limit_bytes=VMEM_LIMIT),
        name=name,
    )(q, k, vt)


def _post_kernel(*refs, last):
    if last:
        x_ref, o_ref, mod_ref, n2_ref, wo_ref, wgu_ref, wd_ref, fn_ref, out_ref = refs
    else:
        (x_ref, o_ref, octx_ref, mod_ref, n2_ref, wo_ref, wgu_ref, wd_ref, fn_ref,
         out_ref) = refs
    mod = mod_ref[0]
    g1 = mod[:, 2 * D_MODEL:3 * D_MODEL]
    sh2 = mod[:, 3 * D_MODEL:4 * D_MODEL]
    s2 = mod[:, 4 * D_MODEL:5 * D_MODEL]
    g2 = mod[:, 5 * D_MODEL:6 * D_MODEL]
    o = o_ref[0] if last else jnp.where(pl.program_id(1) == 0, octx_ref[0], o_ref[0])
    x1 = x_ref[0] + g1 * jnp.dot(o, wo_ref[...], preferred_element_type=_F32)
    h2 = (_rms(x1, n2_ref[...]) * (1.0 + s2) + sh2).astype(_BF16)
    y = None
    for c in range(D_FF // FF_CHUNK):
        lo = c * FF_CHUNK
        gate = jnp.dot(h2, wgu_ref[:, lo:lo + FF_CHUNK], preferred_element_type=_F32)
        up = jnp.dot(h2, wgu_ref[:, D_FF + lo:D_FF + lo + FF_CHUNK],
                     preferred_element_type=_F32)
        act = (gate * jax.nn.sigmoid(gate) * up).astype(_BF16)
        part = jnp.dot(act, wd_ref[lo:lo + FF_CHUNK, :], preferred_element_type=_F32)
        y = part if y is None else y + part
    x2 = x1 + g2 * y
    out_ref[0] = _rms(x2, fn_ref[...]) if last else x2


def _post(x_all, o, o_ctx, mod, norm2, w_o, w_gu, w_down, final_norm):
    last = o_ctx is None
    first_tile = 1 if last else 0
    nt = N_TILES - first_tile
    tok_in = lambda b, ti: (b, ti + first_tile, 0)
    if last:
        mod_idx = lambda b, ti: (b, 0, 0)
        o_specs = [pl.BlockSpec((1, TM, D_MODEL), lambda b, ti: (b, ti, 0))]
        o_args = [o]
    else:
        mod_idx = lambda b, ti: (_mod_row_index(b, ti), 0, 0)
        o_specs = [pl.BlockSpec((1, TM, D_MODEL), lambda b, ti: (b, jnp.maximum(ti - 1, 0), 0)),
                   pl.BlockSpec((1, TM, D_MODEL), lambda b, ti: (b, 0, 0))]
        o_args = [o, o_ctx]
    return pl.pallas_call(
        functools.partial(_post_kernel, last=last),
        grid=(BATCH, nt),
        in_specs=[pl.BlockSpec((1, TM, D_MODEL), tok_in)] + o_specs + [
            pl.BlockSpec((1, 1, 6 * D_MODEL), mod_idx),
            _resident((1, D_MODEL)),
            _resident((D_MODEL, D_MODEL)),
            _resident((D_MODEL, 2 * D_FF)),
            _resident((D_FF, D_MODEL)),
            _resident((1, D_MODEL)),
        ],
        out_specs=pl.BlockSpec((1, TM, D_MODEL), lambda b, ti: (b, ti, 0)),
        out_shape=jax.ShapeDtypeStruct((BATCH, nt * TM, D_MODEL), _F32),
        compiler_params=pltpu.CompilerParams(
            dimension_semantics=("parallel", "parallel"), vmem_limit_bytes=VMEM_LIMIT),
        name="post_last" if last else "post",
    )(x_all, *o_args, mod, norm2, w_o, w_gu, w_down, final_norm)


def _axial_angles(rot_dim):
    n = rot_dim // 4
    inv = ROPE_THETA ** (-jnp.arange(n, dtype=_F32) / n)
    rows = SEQ // GRID_W
    row = jnp.repeat(jnp.arange(rows, dtype=_F32), GRID_W)
    col = jnp.tile(jnp.arange(GRID_W, dtype=_F32), rows)
    return jnp.concatenate([row[:, None] * inv, col[:, None] * inv], axis=-1)


def _with_context_rows(cos, sin_lo, sin_hi):
    pad = lambda t, v: jnp.concatenate([jnp.full((CTX_LEN, LANES), v, _F32), t], axis=0)
    return pad(cos, 1.0), pad(sin_lo, 0.0), pad(sin_hi, 0.0)


def _gqa_tables():
    ang = _axial_angles(GQA_HEAD_DIM)
    cos, sin = jnp.cos(ang), jnp.sin(ang)
    zero = jnp.zeros_like(sin)
    reps = LANES // GQA_HEAD_DIM
    cos_t = jnp.tile(jnp.concatenate([cos, cos], -1), (1, reps))
    sin_lo = jnp.tile(jnp.concatenate([-sin, zero], -1), (1, reps))
    sin_hi = jnp.tile(jnp.concatenate([zero, sin], -1), (1, reps))
    return _with_context_rows(cos_t, sin_lo, sin_hi)


def _mla_tables():
    ang = _axial_angles(MLA_ROPE_DIM)
    cos, sin = jnp.cos(ang), jnp.sin(ang)
    zero = jnp.zeros_like(sin)
    ones_nope = jnp.ones((SEQ, MLA_NOPE_DIM), _F32)
    zeros_nope = jnp.zeros((SEQ, MLA_NOPE_DIM), _F32)
    tail = LANES - MLA_QK_DIM
    cos_t = jnp.concatenate([ones_nope, cos, cos, jnp.ones((SEQ, tail), _F32)], -1)
    sin_lo = jnp.concatenate([zeros_nope, -sin, zero, jnp.zeros((SEQ, tail), _F32)], -1)
    sin_hi = jnp.concatenate([zeros_nope, zero, sin, jnp.zeros((SEQ, tail), _F32)], -1)
    return _with_context_rows(cos_t, sin_lo, sin_hi)


def _mla_weight_layout(w_in, q_b, kv_b):
    w_in_p = jnp.pad(w_in, ((0, 0), (0, W_IN_PAD - w_in.shape[1])))
    qb = q_b.reshape(MLA_Q_RANK, MLA_HEADS, MLA_QK_DIM)
    qb = jnp.pad(qb, ((0, 0), (0, 0), (0, LANES - MLA_QK_DIM)))
    qb = qb.reshape(MLA_Q_RANK, MLA_HEADS * LANES)
    kvb = kv_b.reshape(MLA_KV_RANK, MLA_HEADS, MLA_NOPE_DIM + MLA_V_DIM)
    kb = jnp.pad(kvb[:, :, :MLA_NOPE_DIM], ((0, 0), (0, 0), (0, LANES - MLA_NOPE_DIM)))
    kb = kb.reshape(MLA_KV_RANK, MLA_HEADS * LANES)
    vb = kvb[:, :, MLA_NOPE_DIM:].reshape(MLA_KV_RANK, MLA_HEADS * MLA_V_DIM)
    return (w_in_p.astype(_BF16), qb.astype(_BF16),
            jnp.concatenate([kb, vb], axis=-1).astype(_BF16))


def kernel(x, c, ctx, c_ctx, norm1, norm2, w_mod, b_mod, w_gate_up, w_down, gqa_w_qkv,
           gqa_q_norm, gqa_k_norm, gqa_w_o, mla_w_in, mla_q_a_norm, mla_q_b, mla_kv_a_norm,
           mla_kv_b, mla_w_o, final_norm):
    assert x.shape == (BATCH, SEQ, D_MODEL) and ctx.shape == (BATCH, CTX_LEN, D_MODEL)

    cvec = jnp.zeros((MOD_ROWS, D_MODEL), _F32).at[:BATCH].set(c).at[CTX_ROW].set(c_ctx)
    mod = _modulation(cvec, w_mod, b_mod).reshape(DEPTH, MOD_ROWS, 1, 6 * D_MODEL)

    x_all = jnp.concatenate([ctx, x], axis=1)
    fn = final_norm.reshape(1, D_MODEL)
    w_gu = w_gate_up.astype(_BF16)
    w_dn = w_down.astype(_BF16)

    seg = jnp.kron(jnp.eye(LANES // GQA_HEAD_DIM, dtype=_F32),
                   jnp.full((GQA_HEAD_DIM, GQA_HEAD_DIM), 1.0 / GQA_HEAD_DIM, _F32))
    reps = LANES // GQA_HEAD_DIM
    q, k, vt = _gqa_proj(
        x_all, mod[0], norm1[0].reshape(1, D_MODEL), gqa_w_qkv[0].astype(_BF16),
        jnp.tile(gqa_q_norm[0], reps).reshape(1, LANES),
        jnp.tile(gqa_k_norm[0], reps).reshape(1, LANES), seg, *_gqa_tables())
    o = _attention(q, k, vt, shared_kv=True, context=False)
    o_ctx = _attention(q, k, vt, shared_kv=True, context=True)
    x_all = _post(x_all, o, o_ctx, mod[0], norm2[0].reshape(1, D_MODEL),
                  gqa_w_o[0].astype(_BF16), w_gu[0], w_dn[0], fn)

    w_in, q_b, kv_b = _mla_weight_layout(mla_w_in[0], mla_q_b[0], mla_kv_b[0])
    q, k, vt = _mla_proj(
        x_all, mod[1], norm1[1].reshape(1, D_MODEL), w_in,
        mla_q_a_norm[0].reshape(1, MLA_Q_RANK), mla_kv_a_norm[0].reshape(1, MLA_KV_RANK),
        q_b, kv_b, *_mla_tables())
    o = _attention(q, k, vt, shared_kv=False, context=False)
    return _post(x_all, o, None, mod[1], norm2[1].reshape(1, D_MODEL),
                 mla_w_o[0].astype(_BF16), w_gu[1], w_dn[1], fn)
```
